```python
import math
import jax, jax.numpy as jnp
from jax import lax
import numpy as np

D_MODEL = 1024
BATCH = 4
SEQ = 8192
DEPTH = 4

HEAD_DIM = 64
N_HEADS = 8
N_KV_HEADS = 2
GQA_GROUP = N_HEADS // N_KV_HEADS
ATTN_WIDTH = N_HEADS * HEAD_DIM
KV_WIDTH = N_KV_HEADS * HEAD_DIM
CONV_WIDTH = D_MODEL - ATTN_WIDTH
MIX_WIDTH = ATTN_WIDTH + CONV_WIDTH
IN_WIDTH = ATTN_WIDTH + 2 * KV_WIDTH + 2 * CONV_WIDTH
WINDOW = 128
BLOCK = 128
CONV_KERNEL = 31
NUM_BUCKETS = 32
MAX_DISTANCE = 128
D_FF = 4 * D_MODEL
EPS = 1e-6
NEG = -1e30

kernel_name = "hymba_conformer_swa_sink_hybrid"


def rms_norm(x, g):
    xf = x.astype(jnp.float32)
    y = xf * lax.rsqrt(jnp.mean(xf * xf, axis=-1, keepdims=True) + EPS)
    return (y * g.astype(jnp.float32)).astype(x.dtype)


def layer_norm(x, g, b):
    xf = x.astype(jnp.float32)
    mu = jnp.mean(xf, axis=-1, keepdims=True)
    var = jnp.mean(jnp.square(xf - mu), axis=-1, keepdims=True)
    y = (xf - mu) * lax.rsqrt(var + EPS)
    return (y * g.astype(jnp.float32) + b.astype(jnp.float32)).astype(x.dtype)


def t5_causal_bucket(n):
    n = np.asarray(n)
    max_exact = NUM_BUCKETS // 2
    large = max_exact + (np.log(np.maximum(n, 1) / max_exact)
                         / np.log(MAX_DISTANCE / max_exact)
                         * (NUM_BUCKETS - max_exact)).astype(np.int32)
    large = np.minimum(large, NUM_BUCKETS - 1)
    return np.where(n < max_exact, n, large).astype(np.int32)


def band_structure(seq_len):
    n_blocks = seq_len // BLOCK
    qi = np.arange(BLOCK)[:, None]
    kj = np.arange(2 * BLOCK)[None, :]
    dist = qi + BLOCK - kj
    in_window = (dist >= 0) & (dist < WINDOW)
    bucket = t5_causal_bucket(np.clip(dist, 0, None))
    k_abs = (np.arange(n_blocks)[:, None] - 1) * BLOCK + np.arange(2 * BLOCK)[None, :]
    valid = in_window[None, :, :] & (k_abs >= 0)[:, None, :]
    return bucket, valid


def sliding_window_gqa(q, k, v, sinks, rel_bias):
    B, T = q.shape[0], q.shape[1]
    nb = T // BLOCK
    bucket, valid = band_structure(T)
    bias = jnp.transpose(rel_bias[bucket].astype(jnp.float32), (2, 0, 1))
    bias = bias.reshape(N_KV_HEADS, GQA_GROUP, BLOCK, 2 * BLOCK)

    qb = q.reshape(B, nb, BLOCK, N_KV_HEADS, GQA_GROUP, HEAD_DIM)

    def band(t):
        tp = jnp.pad(t, ((0, 0), (BLOCK, 0), (0, 0), (0, 0)))
        prev = tp[:, :T].reshape(B, nb, BLOCK, N_KV_HEADS, HEAD_DIM)
        cur = t.reshape(B, nb, BLOCK, N_KV_HEADS, HEAD_DIM)
        return jnp.concatenate([prev, cur], axis=2)

    kb, vb = band(k), band(v)
    scale = 1.0 / math.sqrt(HEAD_DIM)
    s = jnp.einsum('bnqkgd,bnskd->bnkgqs', qb, kb).astype(jnp.float32) * scale
    s = s + bias[None, None]
    s = jnp.where(jnp.asarray(valid)[None, :, None, None, :, :], s, NEG)
    sink = sinks.astype(jnp.float32).reshape(N_KV_HEADS, GQA_GROUP)[None, None, :, :, None, None]
    m = jnp.maximum(jnp.max(s, axis=-1, keepdims=True), sink)
    p = jnp.exp(s - m)
    p = p / (jnp.sum(p, axis=-1, keepdims=True) + jnp.exp(sink - m))
    o = jnp.einsum('bnkgqs,bnskd->bnqkgd', p.astype(v.dtype), vb)
    return o.reshape(B, T, ATTN_WIDTH)


def conformer_conv(u, gate, conv_w, conv_b, ln_g, ln_b):
    h = u * jax.nn.sigmoid(gate)
    hp = jnp.pad(h, ((0, 0), (CONV_KERNEL - 1, 0), (0, 0)))
    y = lax.conv_general_dilated(
        hp, conv_w[:, None, :].astype(h.dtype), window_strides=(1,), padding='VALID',
        dimension_numbers=('NWC', 'WIO', 'NWC'), feature_group_count=CONV_WIDTH)
    y = y + conv_b
    y = layer_norm(y, ln_g, ln_b)
    return jax.nn.silu(y)


def setup_inputs(seed: int = 0) -> dict:
    key = jax.random.key(seed)
    ks = jax.random.split(key, 20)
    f32 = jnp.float32
    nrm = lambda k, shape, s: (jax.random.normal(k, shape, f32) * s).astype(f32)
    return {
        "x": nrm(ks[0], (BATCH, SEQ, D_MODEL), 1.0),
        "rel_bias": nrm(ks[1], (NUM_BUCKETS, N_HEADS), 0.5),
        "norm_mix_g": 1.0 + nrm(ks[2], (DEPTH, D_MODEL), 0.02),
        "w_in": nrm(ks[3], (DEPTH, D_MODEL, IN_WIDTH), D_MODEL ** -0.5),
        "q_norm_g": 1.0 + nrm(ks[4], (DEPTH, HEAD_DIM), 0.02),
        "k_norm_g": 1.0 + nrm(ks[5], (DEPTH, HEAD_DIM), 0.02),
        "sinks": nrm(ks[6], (DEPTH, N_HEADS), 0.5),
        "conv_w": nrm(ks[7], (DEPTH, CONV_KERNEL, CONV_WIDTH), CONV_KERNEL ** -0.5),
        "conv_b": nrm(ks[8], (DEPTH, CONV_WIDTH), 0.02),
        "conv_ln_g": 1.0 + nrm(ks[9], (DEPTH, CONV_WIDTH), 0.02),
        "conv_ln_b": nrm(ks[10], (DEPTH, CONV_WIDTH), 0.02),
        "attn_out_g": 1.0 + nrm(ks[11], (DEPTH, ATTN_WIDTH), 0.02),
        "conv_out_g": 1.0 + nrm(ks[12], (DEPTH, CONV_WIDTH), 0.02),
        "w_out": nrm(ks[13], (DEPTH, MIX_WIDTH, D_MODEL), (MIX_WIDTH * 2 * DEPTH) ** -0.5),
        "norm_mlp_g": 1.0 + nrm(ks[14], (DEPTH, D_MODEL), 0.02),
        "w_mlp_up": nrm(ks[15], (DEPTH, D_MODEL, D_FF), D_MODEL ** -0.5),
        "w_mlp_down": nrm(ks[16], (DEPTH, D_FF, D_MODEL), (D_FF * 2 * DEPTH) ** -0.5),
    }


def reference(x, rel_bias, norm_mix_g, w_in, q_norm_g, k_norm_g, sinks, conv_w, conv_b,
              conv_ln_g, conv_ln_b, attn_out_g, conv_out_g, w_out, norm_mlp_g,
              w_mlp_up, w_mlp_down):
    B, T, _ = x.shape
    splits = [ATTN_WIDTH, ATTN_WIDTH + KV_WIDTH, ATTN_WIDTH + 2 * KV_WIDTH,
              ATTN_WIDTH + 2 * KV_WIDTH + CONV_WIDTH]
    for l in range(DEPTH):
        h = rms_norm(x, norm_mix_g[l])
        z = h @ w_in[l]
        q, k, v, u, gate = jnp.split(z, splits, axis=-1)
        q = rms_norm(q.reshape(B, T, N_HEADS, HEAD_DIM), q_norm_g[l])
        k = rms_norm(k.reshape(B, T, N_KV_HEADS, HEAD_DIM), k_norm_g[l])
        v = v.reshape(B, T, N_KV_HEADS, HEAD_DIM)
        a = sliding_window_gqa(q, k, v, sinks[l], rel_bias)
        c = conformer_conv(u, gate, conv_w[l], conv_b[l], conv_ln_g[l], conv_ln_b[l])
        mix = jnp.concatenate([rms_norm(a, attn_out_g[l]), rms_norm(c, conv_out_g[l])], axis=-1)
        x = x + mix @ w_out[l]
        h = rms_norm(x, norm_mlp_g[l])
        x = x + jnp.square(jax.nn.relu(h @ w_mlp_up[l])) @ w_mlp_down[l]
    return x
```

```python
import functools

import numpy as np
import jax
import jax.numpy as jnp
from jax import lax
from jax.experimental import pallas as pl
from jax.experimental.pallas import tpu as pltpu

D_MODEL = 1024
HEAD_DIM = 64
N_HEADS = 8
N_KV_HEADS = 2
GQA_GROUP = N_HEADS // N_KV_HEADS
ATTN_WIDTH = N_HEADS * HEAD_DIM
KV_WIDTH = N_KV_HEADS * HEAD_DIM
CONV_WIDTH = D_MODEL - ATTN_WIDTH
MIX_WIDTH = ATTN_WIDTH + CONV_WIDTH
IN_WIDTH = ATTN_WIDTH + 2 * KV_WIDTH + 2 * CONV_WIDTH
WINDOW = 128
BLOCK = 128
CONV_KERNEL = 31
NUM_BUCKETS = 32
MAX_DISTANCE = 128
D_FF = 4 * D_MODEL
EPS = 1e-6
NEG = -1e30

LANES = 128
HALO = 32
TM_PROJ = 512
TQ_MIX = 256
TM_MLP = 512
CONV_ROWS = 32
VMEM_LIMIT = 56 * 1024 * 1024

F32 = jnp.float32
BF16 = jnp.bfloat16


def _bucket_table():
    qi = np.arange(BLOCK)[:, None]
    kj = np.arange(2 * BLOCK)[None, :]
    n = np.clip(qi + BLOCK - kj, 0, None)
    max_exact = NUM_BUCKETS // 2
    large = max_exact + (np.log(np.maximum(n, 1) / max_exact)
                         / np.log(MAX_DISTANCE / max_exact)
                         * (NUM_BUCKETS - max_exact)).astype(np.int32)
    large = np.minimum(large, NUM_BUCKETS - 1)
    return np.where(n < max_exact, n, large).astype(np.int32)


def _block_diag_ones(width, seg):
    r = np.arange(width)
    return (r[:, None] // seg == r[None, :] // seg).astype(np.float32)


def _bias_kernel(rel_ref, bucket_ref, out_ref):
    b = bucket_ref[...]
    for h in range(N_HEADS):
        acc = jnp.zeros((BLOCK, 2 * BLOCK), F32)
        for n in range(NUM_BUCKETS):
            acc = jnp.where(b == n, rel_ref[n, h], acc)
        out_ref[h] = acc


def _bias_table(rel_bias):
    bucket = jnp.asarray(_bucket_table())
    return pl.pallas_call(
        _bias_kernel,
        out_shape=jax.ShapeDtypeStruct((N_HEADS, BLOCK, 2 * BLOCK), F32),
        in_specs=[pl.BlockSpec(memory_space=pltpu.SMEM),
                  pl.BlockSpec(memory_space=pltpu.VMEM)],
        out_specs=pl.BlockSpec(memory_space=pltpu.VMEM),
        name="bias_table",
    )(rel_bias, bucket)


def _segment_mean_sq(z, ones_ref, seg):
    sq = z * z
    hi = sq.astype(BF16)
    lo = (sq - hi.astype(F32)).astype(BF16)
    ones = ones_ref[...]
    tot = (jnp.dot(hi, ones, preferred_element_type=F32)
           + jnp.dot(lo, ones, preferred_element_type=F32))
    return tot * (1.0 / seg)


def _inproj_kernel(x_ref, g_ref, w_ref, ones_q_ref, ones_k_ref, gq_ref, gk_ref,
                   q_ref, k_ref, v_ref, h_ref):
    x = x_ref[...]
    ms = jnp.mean(x * x, axis=-1, keepdims=True)
    hn = (x * lax.rsqrt(ms + EPS) * g_ref[...]).astype(BF16)
    tm = x.shape[0]
    low_half = lax.broadcasted_iota(jnp.int32, (tm, LANES), 1) < HEAD_DIM

    zq = jnp.dot(hn, w_ref[:, 0:ATTN_WIDTH], preferred_element_type=F32)
    qn = zq * lax.rsqrt(_segment_mean_sq(zq, ones_q_ref, HEAD_DIM) + EPS) * gq_ref[...]
    for j in range(N_HEADS // 2):
        pair = qn[:, j * LANES:(j + 1) * LANES]
        q_ref[:, (2 * j) * LANES:(2 * j + 1) * LANES] = jnp.where(low_half, pair, 0.0).astype(BF16)
        q_ref[:, (2 * j + 1) * LANES:(2 * j + 2) * LANES] = jnp.where(low_half, 0.0, pair).astype(BF16)

    c0 = ATTN_WIDTH
    zk = jnp.dot(hn, w_ref[:, c0:c0 + KV_WIDTH], preferred_element_type=F32)
    kn = zk * lax.rsqrt(_segment_mean_sq(zk, ones_k_ref, HEAD_DIM) + EPS) * gk_ref[...]
    kr = pltpu.roll(kn, HEAD_DIM, axis=1)
    k_ref[:, 0:LANES] = jnp.where(low_half, kn, kr).astype(BF16)
    k_ref[:, LANES:2 * LANES] = jnp.where(low_half, kr, kn).astype(BF16)

    c0 += KV_WIDTH
    zv = jnp.dot(hn, w_ref[:, c0:c0 + KV_WIDTH], preferred_element_type=F32)
    vr = pltpu.roll(zv, HEAD_DIM, axis=1)
    v_ref[:, 0:LANES] = jnp.where(low_half, zv, vr).astype(BF16)
    v_ref[:, LANES:2 * LANES] = jnp.where(low_half, vr, zv).astype(BF16)

    c0 += KV_WIDTH
    u = jnp.dot(hn, w_ref[:, c0:c0 + CONV_WIDTH], preferred_element_type=F32)
    c0 += CONV_WIDTH
    gate = jnp.dot(hn, w_ref[:, c0:c0 + CONV_WIDTH], preferred_element_type=F32)
    h_ref[...] = u * (1.0 / (1.0 + jnp.exp(-gate)))


def _inproj(x, g, w_bf16, gq, gk):
    B, T, _ = x.shape
    tm = TM_PROJ
    ones_q = jnp.asarray(_block_diag_ones(ATTN_WIDTH, HEAD_DIM), BF16)
    ones_k = jnp.asarray(_block_diag_ones(KV_WIDTH, HEAD_DIM), BF16)
    const = lambda b, i: (0, 0)
    row = lambda b, i: (b, i, 0)
    return pl.pallas_call(
        _inproj_kernel,
        out_shape=(jax.ShapeDtypeStruct((B, T, N_HEADS * LANES), BF16),
                   jax.ShapeDtypeStruct((B, T, N_KV_HEADS * LANES), BF16),
                   jax.ShapeDtypeStruct((B, T, N_KV_HEADS * LANES), BF16),
                   jax.ShapeDtypeStruct((B, T, CONV_WIDTH), F32)),
        grid=(B, T // tm),
        in_specs=[pl.BlockSpec((None, tm, D_MODEL), row),
                  pl.BlockSpec((1, D_MODEL), const),
                  pl.BlockSpec((D_MODEL, IN_WIDTH), const),
                  pl.BlockSpec((ATTN_WIDTH, ATTN_WIDTH), const),
                  pl.BlockSpec((KV_WIDTH, KV_WIDTH), const),
                  pl.BlockSpec((1, ATTN_WIDTH), const),
                  pl.BlockSpec((1, KV_WIDTH), const)],
        out_specs=(pl.BlockSpec((None, tm, N_HEADS * LANES), row),
                   pl.BlockSpec((None, tm, N_KV_HEADS * LANES), row),
                   pl.BlockSpec((None, tm, N_KV_HEADS * LANES), row),
                   pl.BlockSpec((None, tm, CONV_WIDTH), row)),
        compiler_params=pltpu.CompilerParams(
            dimension_semantics=("arbitrary", "arbitrary"), vmem_limit_bytes=VMEM_LIMIT),
        name="inproj",
    )(x, g, w_bf16, ones_q, ones_k, gq, gk)


def _rms_rows(y, g):
    ms = jnp.mean(y * y, axis=-1, keepdims=True)
    return y * lax.rsqrt(ms + EPS) * g


def _mix_kernel(sinks_ref, q_ref, kc_ref, kp_ref, vc_ref, vp_ref, hc_ref, hp_ref, x_ref,
                bias_ref, cw_ref, cb_ref, lng_ref, lnb_ref, ag_ref, cg_ref, wo_ref,
                o_ref, hbuf_ref, mix_ref):
    tq = q_ref.shape[0]
    first = pl.program_id(1) == 0

    qi = lax.broadcasted_iota(jnp.int32, (BLOCK, 2 * BLOCK), 0)
    kj = lax.broadcasted_iota(jnp.int32, (BLOCK, 2 * BLOCK), 1)
    in_window = (kj > qi) & (kj <= qi + BLOCK)
    low_half = lax.broadcasted_iota(jnp.int32, (BLOCK, LANES), 1) < HEAD_DIM
    for sb in range(tq // BLOCK):
        r0 = sb * BLOCK
        if sb == 0:
            k_prev, v_prev = kp_ref[...], vp_ref[...]
            valid = in_window & ((kj >= BLOCK) | jnp.logical_not(first))
        else:
            k_prev, v_prev = kc_ref[r0 - BLOCK:r0, :], vc_ref[r0 - BLOCK:r0, :]
            valid = in_window
        kcat = jnp.concatenate([k_prev, kc_ref[r0:r0 + BLOCK, :]], axis=0)
        vcat = jnp.concatenate([v_prev, vc_ref[r0:r0 + BLOCK, :]], axis=0)
        valid4 = jnp.concatenate([valid] * GQA_GROUP, axis=0)
        heads = []
        for kvh in range(N_KV_HEADS):
            kk = kcat[:, kvh * LANES:(kvh + 1) * LANES]
            vv = vcat[:, kvh * LANES:(kvh + 1) * LANES]
            h0 = kvh * GQA_GROUP
            qs = jnp.concatenate(
                [q_ref[r0:r0 + BLOCK, (h0 + g) * LANES:(h0 + g + 1) * LANES]
                 for g in range(GQA_GROUP)], axis=0)
            s = lax.dot_general(qs, kk, (((1,), (1,)), ((), ())),
                                preferred_element_type=F32)
            s = s + bias_ref[h0:h0 + GQA_GROUP].reshape(GQA_GROUP * BLOCK, 2 * BLOCK)
            s = jnp.where(valid4, s, NEG)
            sink = jnp.concatenate(
                [jnp.full((BLOCK, 1), sinks_ref[h0 + g], F32) for g in range(GQA_GROUP)], axis=0)
            m = jnp.maximum(jnp.max(s, axis=-1, keepdims=True), sink)
            p = jnp.exp(s - m)
            denom = jnp.sum(p, axis=-1, keepdims=True) + jnp.exp(sink - m)
            o = jnp.dot(p.astype(BF16), vv, preferred_element_type=F32)
            o = o * (1.0 / denom)
            heads += [o[g * BLOCK:(g + 1) * BLOCK] for g in range(GQA_GROUP)]
        a = jnp.concatenate(
            [jnp.where(low_half, heads[2 * j], heads[2 * j + 1]) for j in range(N_HEADS // 2)],
            axis=-1)
        mix_ref[r0:r0 + BLOCK, 0:ATTN_WIDTH] = _rms_rows(a, ag_ref[...]).astype(BF16)

    hbuf_ref[0:HALO, :] = jnp.where(first, 0.0, hp_ref[...])
    hbuf_ref[HALO:HALO + tq, :] = hc_ref[...]
    off = HALO - (CONV_KERNEL - 1)
    for c in range(tq // CONV_ROWS):
        r0 = c * CONV_ROWS
        acc = jnp.broadcast_to(cb_ref[...], (CONV_ROWS, CONV_WIDTH))
        for t in range(CONV_KERNEL):
            acc = acc + cw_ref[t:t + 1, :] * hbuf_ref[r0 + off + t:r0 + off + t + CONV_ROWS, :]
        mu = jnp.mean(acc, axis=-1, keepdims=True)
        d = acc - mu
        var = jnp.mean(d * d, axis=-1, keepdims=True)
        y = d * lax.rsqrt(var + EPS) * lng_ref[...] + lnb_ref[...]
        y = y * (1.0 / (1.0 + jnp.exp(-y)))
        mix_ref[r0:r0 + CONV_ROWS, ATTN_WIDTH:MIX_WIDTH] = _rms_rows(y, cg_ref[...]).astype(BF16)

    o_ref[...] = x_ref[...] + jnp.dot(mix_ref[...], wo_ref[...], preferred_element_type=F32)


def _mix(sinks, q, k, v, h, x, bias, cw, cb, lng, lnb, ag, cg, wo_bf16):
    B, T, _ = x.shape
    tq = TQ_MIX
    const2 = lambda b, i: (0, 0)
    const3 = lambda b, i: (0, 0, 0)
    row = lambda b, i: (b, i, 0)
    prev_blk = lambda b, i: (b, jnp.maximum(i * (tq // BLOCK) - 1, 0), 0)
    prev_halo = lambda b, i: (b, jnp.maximum(i * (tq // HALO) - 1, 0), 0)
    kvw = N_KV_HEADS * LANES
    return pl.pallas_call(
        _mix_kernel,
        out_shape=jax.ShapeDtypeStruct((B, T, D_MODEL), F32),
        grid=(B, T // tq),
        in_specs=[pl.BlockSpec(memory_space=pltpu.SMEM),
                  pl.BlockSpec((None, tq, N_HEADS * LANES), row),
                  pl.BlockSpec((None, tq, kvw), row),
                  pl.BlockSpec((None, BLOCK, kvw), prev_blk),
                  pl.BlockSpec((None, tq, kvw), row),
                  pl.BlockSpec((None, BLOCK, kvw), prev_blk),
                  pl.BlockSpec((None, tq, CONV_WIDTH), row),
                  pl.BlockSpec((None, HALO, CONV_WIDTH), prev_halo),
                  pl.BlockSpec((None, tq, D_MODEL), row),
                  pl.BlockSpec((N_HEADS, BLOCK, 2 * BLOCK), const3),
                  pl.BlockSpec((CONV_KERNEL, CONV_WIDTH), const2),
                  pl.BlockSpec((1, CONV_WIDTH), const2),
                  pl.BlockSpec((1, CONV_WIDTH), const2),
                  pl.BlockSpec((1, CONV_WIDTH), const2),
                  pl.BlockSpec((1, ATTN_WIDTH), const2),
                  pl.BlockSpec((1, CONV_WIDTH), const2),
                  pl.BlockSpec((MIX_WIDTH, D_MODEL), const2)],
        out_specs=pl.BlockSpec((None, tq, D_MODEL), row),
        scratch_shapes=[pltpu.VMEM((HALO + tq, CONV_WIDTH), F32),
                        pltpu.VMEM((tq, MIX_WIDTH), BF16)],
        compiler_params=pltpu.CompilerParams(
            dimension_semantics=("arbitrary", "arbitrary"), vmem_limit_bytes=VMEM_LIMIT),
        name="mix",
    )(sinks, q, k, k, v, v, h, h, x, bias, cw, cb, lng, lnb, ag, cg, wo_bf16)


def _mlp_kernel(x_ref, g_ref, wu_ref, wd_ref, o_ref):
    x = x_ref[...]
    hn = _rms_rows(x, g_ref[...]).astype(BF16)
    u = jnp.dot(hn, wu_ref[...], preferred_element_type=F32)
    r = jnp.maximum(u, 0.0)
    r = (r * r).astype(BF16)
    o_ref[...] = x + jnp.dot(r, wd_ref[...], preferred_element_type=F32)


def _mlp(x, g, wu_bf16, wd_bf16):
    B, T, _ = x.shape
    tm = TM_MLP
    const = lambda b, i: (0, 0)
    row = lambda b, i: (b, i, 0)
    return pl.pallas_call(
        _mlp_kernel,
        out_shape=jax.ShapeDtypeStruct((B, T, D_MODEL), F32),
        grid=(B, T // tm),
        in_specs=[pl.BlockSpec((None, tm, D_MODEL), row),
                  pl.BlockSpec((1, D_MODEL), const),
                  pl.BlockSpec((D_MODEL, D_FF), const, pipeline_mode=pl.Buffered(1)),
                  pl.BlockSpec((D_FF, D_MODEL), const, pipeline_mode=pl.Buffered(1))],
        out_specs=pl.BlockSpec((None, tm, D_MODEL), row),
        compiler_params=pltpu.CompilerParams(
            dimension_semantics=("arbitrary", "arbitrary"), vmem_limit_bytes=VMEM_LIMIT),
        name="mlp",
    )(x, g, wu_bf16, wd_bf16)


def kernel(x, rel_bias, norm_mix_g, w_in, q_norm_g, k_norm_g, sinks, conv_w, conv_b,
           conv_ln_g, conv_ln_b, attn_out_g, conv_out_g, w_out, norm_mlp_g,
           w_mlp_up, w_mlp_down):
    depth = w_in.shape[0]
    bias = _bias_table(rel_bias)
    row = lambda a: a[None, :]
    for l in range(depth):
        gq = row(jnp.tile(q_norm_g[l], N_HEADS) * (HEAD_DIM ** -0.5))
        gk = row(jnp.tile(k_norm_g[l], N_KV_HEADS))
        q, k, v, h = _inproj(x, row(norm_mix_g[l]), w_in[l].astype(BF16), gq, gk)
        x = _mix(sinks[l], q, k, v, h, x, bias, conv_w[l], row(conv_b[l]),
                 row(conv_ln_g[l]), row(conv_ln_b[l]), row(attn_out_g[l]),
                 row(conv_out_g[l]), w_out[l].astype(BF16))
        x = _mlp(x, row(norm_mlp_g[l]), w_mlp_up[l].astype(BF16), w_mlp_down[l].astype(BF16))
    return x
```

```python
import functools

import numpy as np
import jax
import jax.numpy as jnp
from jax import lax
from jax.experimental import pallas as pl
from jax.experimental.pallas import tpu as pltpu

D_MODEL = 1024
HEAD_DIM = 64
N_HEADS = 8
N_KV_HEADS = 2
GQA_GROUP = N_HEADS // N_KV_HEADS
ATTN_WIDTH = N_HEADS * HEAD_DIM
KV_WIDTH = N_KV_HEADS * HEAD_DIM
CONV_WIDTH = D_MODEL - ATTN_WIDTH
MIX_WIDTH = ATTN_WIDTH + CONV_WIDTH
IN_WIDTH = ATTN_WIDTH + 2 * KV_WIDTH + 2 * CONV_WIDTH
WINDOW = 128
BLOCK = 128
CONV_KERNEL = 31
NUM_BUCKETS = 32
MAX_DISTANCE = 128
D_FF = 4 * D_MODEL
EPS = 1e-6
NEG = -1e30

LANES = 128
SUBLANES = 8
HALO = 32
TM_PROJ = 512
TQ_MIX = 256
TM_MLP = 512
CONV_ROWS = 32
VMEM_LIMIT = 56 * 1024 * 1024

F32 = jnp.float32
BF16 = jnp.bfloat16


def _bucket_table():
    qi = np.arange(BLOCK)[:, None]
    kj = np.arange(2 * BLOCK)[None, :]
    n = np.clip(qi + BLOCK - kj, 0, None)
    max_exact = NUM_BUCKETS // 2
    large = max_exact + (np.log(np.maximum(n, 1) / max_exact)
                         / np.log(MAX_DISTANCE / max_exact)
                         * (NUM_BUCKETS - max_exact)).astype(np.int32)
    large = np.minimum(large, NUM_BUCKETS - 1)
    return np.where(n < max_exact, n, large).astype(np.int32)


def _block_diag_ones(width, seg):
    r = np.arange(width)
    return (r[:, None] // seg == r[None, :] // seg).astype(np.float32)


def _bias_kernel(rel_ref, bucket_ref, out_ref):
    b = bucket_ref[...]
    for h in range(N_HEADS):
        acc = jnp.zeros((BLOCK, 2 * BLOCK), F32)
        for n in range(NUM_BUCKETS):
            acc = jnp.where(b == n, rel_ref[n, h], acc)
        out_ref[h] = acc


def _bias_table(rel_bias):
    bucket = jnp.asarray(_bucket_table())
    return pl.pallas_call(
        _bias_kernel,
        out_shape=jax.ShapeDtypeStruct((N_HEADS, BLOCK, 2 * BLOCK), F32),
        in_specs=[pl.BlockSpec(memory_space=pltpu.SMEM),
                  pl.BlockSpec(memory_space=pltpu.VMEM)],
        out_specs=pl.BlockSpec(memory_space=pltpu.VMEM),
        name="bias_table",
    )(rel_bias, bucket)


def _segment_mean_sq(z, ones_ref, seg):
    sq = z * z
    hi = sq.astype(BF16)
    lo = (sq - hi.astype(F32)).astype(BF16)
    ones = ones_ref[...]
    tot = (jnp.dot(hi, ones, preferred_element_type=F32)
           + jnp.dot(lo, ones, preferred_element_type=F32))
    return tot * (1.0 / seg)


def _inproj_kernel(x_ref, g_ref, w_ref, ones_q_ref, ones_k_ref, gq_ref, gk_ref,
                   q_ref, k_ref, v_ref, h_ref):
    x = x_ref[...]
    ms = jnp.mean(x * x, axis=-1, keepdims=True)
    hn = (x * lax.rsqrt(ms + EPS) * g_ref[...]).astype(BF16)
    tm = x.shape[0]
    low_half = lax.broadcasted_iota(jnp.int32, (tm, LANES), 1) < HEAD_DIM

    zq = jnp.dot(hn, w_ref[:, 0:ATTN_WIDTH], preferred_element_type=F32)
    qn = zq * lax.rsqrt(_segment_mean_sq(zq, ones_q_ref, HEAD_DIM) + EPS) * gq_ref[...]
    for j in range(N_HEADS // 2):
        pair = qn[:, j * LANES:(j + 1) * LANES]
        q_ref[:, (2 * j) * LANES:(2 * j + 1) * LANES] = jnp.where(low_half, pair, 0.0).astype(BF16)
        q_ref[:, (2 * j + 1) * LANES:(2 * j + 2) * LANES] = jnp.where(low_half, 0.0, pair).astype(BF16)

    c0 = ATTN_WIDTH
    zk = jnp.dot(hn, w_ref[:, c0:c0 + KV_WIDTH], preferred_element_type=F32)
    kn = zk * lax.rsqrt(_segment_mean_sq(zk, ones_k_ref, HEAD_DIM) + EPS) * gk_ref[...]
    kr = pltpu.roll(kn, HEAD_DIM, axis=1)
    k_ref[:, 0:LANES] = jnp.where(low_half, kn, kr).astype(BF16)
    k_ref[:, LANES:2 * LANES] = jnp.where(low_half, kr, kn).astype(BF16)

    c0 += KV_WIDTH
    zv = jnp.dot(hn, w_ref[:, c0:c0 + KV_WIDTH], preferred_element_type=F32)
    vr = pltpu.roll(zv, HEAD_DIM, axis=1)
    v_ref[:, 0:LANES] = jnp.where(low_half, zv, vr).astype(BF16)
    v_ref[:, LANES:2 * LANES] = jnp.where(low_half, vr, zv).astype(BF16)

    c0 += KV_WIDTH
    u = jnp.dot(hn, w_ref[:, c0:c0 + CONV_WIDTH], preferred_element_type=F32)
    c0 += CONV_WIDTH
    gate = jnp.dot(hn, w_ref[:, c0:c0 + CONV_WIDTH], preferred_element_type=F32)
    h_ref[...] = u * (1.0 / (1.0 + jnp.exp(-gate)))


def _inproj(x, g, w_bf16, gq, gk):
    B, T, _ = x.shape
    tm = TM_PROJ
    ones_q = jnp.asarray(_block_diag_ones(ATTN_WIDTH, HEAD_DIM), BF16)
    ones_k = jnp.asarray(_block_diag_ones(KV_WIDTH, HEAD_DIM), BF16)
    const = lambda b, i: (0, 0)
    row = lambda b, i: (b, i, 0)
    return pl.pallas_call(
        _inproj_kernel,
        out_shape=(jax.ShapeDtypeStruct((B, T, N_HEADS * LANES), BF16),
                   jax.ShapeDtypeStruct((B, T, N_KV_HEADS * LANES), BF16),
                   jax.ShapeDtypeStruct((B, T, N_KV_HEADS * LANES), BF16),
                   jax.ShapeDtypeStruct((B, T, CONV_WIDTH), F32)),
        grid=(B, T // tm),
        in_specs=[pl.BlockSpec((None, tm, D_MODEL), row),
                  pl.BlockSpec((1, D_MODEL), const),
                  pl.BlockSpec((D_MODEL, IN_WIDTH), const),
                  pl.BlockSpec((ATTN_WIDTH, ATTN_WIDTH), const),
                  pl.BlockSpec((KV_WIDTH, KV_WIDTH), const),
                  pl.BlockSpec((1, ATTN_WIDTH), const),
                  pl.BlockSpec((1, KV_WIDTH), const)],
        out_specs=(pl.BlockSpec((None, tm, N_HEADS * LANES), row),
                   pl.BlockSpec((None, tm, N_KV_HEADS * LANES), row),
                   pl.BlockSpec((None, tm, N_KV_HEADS * LANES), row),
                   pl.BlockSpec((None, tm, CONV_WIDTH), row)),
        compiler_params=pltpu.CompilerParams(
            dimension_semantics=("arbitrary", "arbitrary"), vmem_limit_bytes=VMEM_LIMIT),
        name="inproj",
    )(x, g, w_bf16, ones_q, ones_k, gq, gk)


def _rms_rows(y, g):
    ms = jnp.mean(y * y, axis=-1, keepdims=True)
    return y * lax.rsqrt(ms + EPS) * g


def _mix_kernel(sinks_ref, q_ref, kc_ref, kp_ref, vc_ref, vp_ref, hc_ref, hp_ref, x_ref,
                bias_ref, cw_ref, cb_ref, lng_ref, lnb_ref, ag_ref, cg_ref, wo_ref,
                o_ref, hs_ref, mix_ref):
    tq = q_ref.shape[0]
    first = pl.program_id(1) == 0

    qi = lax.broadcasted_iota(jnp.int32, (BLOCK, 2 * BLOCK), 0)
    kj = lax.broadcasted_iota(jnp.int32, (BLOCK, 2 * BLOCK), 1)
    in_window = (kj > qi) & (kj <= qi + BLOCK)
    low_half = lax.broadcasted_iota(jnp.int32, (BLOCK, LANES), 1) < HEAD_DIM
    for sb in range(tq // BLOCK):
        r0 = sb * BLOCK
        if sb == 0:
            k_prev, v_prev = kp_ref[...], vp_ref[...]
            valid = in_window & ((kj >= BLOCK) | jnp.logical_not(first))
        else:
            k_prev, v_prev = kc_ref[r0 - BLOCK:r0, :], vc_ref[r0 - BLOCK:r0, :]
            valid = in_window
        kcat = jnp.concatenate([k_prev, kc_ref[r0:r0 + BLOCK, :]], axis=0)
        vcat = jnp.concatenate([v_prev, vc_ref[r0:r0 + BLOCK, :]], axis=0)
        valid4 = jnp.concatenate([valid] * GQA_GROUP, axis=0)
        heads = []
        for kvh in range(N_KV_HEADS):
            kk = kcat[:, kvh * LANES:(kvh + 1) * LANES]
            vv = vcat[:, kvh * LANES:(kvh + 1) * LANES]
            h0 = kvh * GQA_GROUP
            qs = jnp.concatenate(
                [q_ref[r0:r0 + BLOCK, (h0 + g) * LANES:(h0 + g + 1) * LANES]
                 for g in range(GQA_GROUP)], axis=0)
            s = lax.dot_general(qs, kk, (((1,), (1,)), ((), ())),
                                preferred_element_type=F32)
            s = s + bias_ref[h0:h0 + GQA_GROUP].reshape(GQA_GROUP * BLOCK, 2 * BLOCK)
            s = jnp.where(valid4, s, NEG)
            sink = jnp.concatenate(
                [jnp.full((BLOCK, 1), sinks_ref[h0 + g], F32) for g in range(GQA_GROUP)], axis=0)
            m = jnp.maximum(jnp.max(s, axis=-1, keepdims=True), sink)
            p = jnp.exp(s - m)
            denom = jnp.sum(p, axis=-1, keepdims=True) + jnp.exp(sink - m)
            o = jnp.dot(p.astype(BF16), vv, preferred_element_type=F32)
            o = o * (1.0 / denom)
            heads += [o[g * BLOCK:(g + 1) * BLOCK] for g in range(GQA_GROUP)]
        a = jnp.concatenate(
            [jnp.where(low_half, heads[2 * j], heads[2 * j + 1]) for j in range(N_HEADS // 2)],
            axis=-1)
        mix_ref[r0:r0 + BLOCK, 0:ATTN_WIDTH] = _rms_rows(a, ag_ref[...]).astype(BF16)

    hs_ref[0, 0:HALO, :] = jnp.where(first, 0.0, hp_ref[...])
    hs_ref[0, HALO:HALO + tq, :] = hc_ref[...]
    span = HALO + tq - SUBLANES
    for s in range(1, SUBLANES):
        hs_ref[s, 0:span, :] = hs_ref[0, s:s + span, :]
    off = HALO - (CONV_KERNEL - 1)
    for c in range(tq // CONV_ROWS):
        r0 = c * CONV_ROWS
        acc = jnp.broadcast_to(cb_ref[...], (CONV_ROWS, CONV_WIDTH))
        for t in range(CONV_KERNEL):
            a, s = divmod(off + t, SUBLANES)
            r = r0 + a * SUBLANES
            acc = acc + cw_ref[t:t + 1, :] * hs_ref[s, r:r + CONV_ROWS, :]
        mu = jnp.mean(acc, axis=-1, keepdims=True)
        d = acc - mu
        var = jnp.mean(d * d, axis=-1, keepdims=True)
        y = d * lax.rsqrt(var + EPS) * lng_ref[...] + lnb_ref[...]
        y = y * (1.0 / (1.0 + jnp.exp(-y)))
        mix_ref[r0:r0 + CONV_ROWS, ATTN_WIDTH:MIX_WIDTH] = _rms_rows(y, cg_ref[...]).astype(BF16)

    o_ref[...] = x_ref[...] + jnp.dot(mix_ref[...], wo_ref[...], preferred_element_type=F32)


def _mix(sinks, q, k, v, h, x, bias, cw, cb, lng, lnb, ag, cg, wo_bf16):
    B, T, _ = x.shape
    tq = TQ_MIX
    const2 = lambda b, i: (0, 0)
    const3 = lambda b, i: (0, 0, 0)
    row = lambda b, i: (b, i, 0)
    prev_blk = lambda b, i: (b, jnp.maximum(i * (tq // BLOCK) - 1, 0), 0)
    prev_halo = lambda b, i: (b, jnp.maximum(i * (tq // HALO) - 1, 0), 0)
    kvw = N_KV_HEADS * LANES
    return pl.pallas_call(
        _mix_kernel,
        out_shape=jax.ShapeDtypeStruct((B, T, D_MODEL), F32),
        grid=(B, T // tq),
        in_specs=[pl.BlockSpec(memory_space=pltpu.SMEM),
                  pl.BlockSpec((None, tq, N_HEADS * LANES), row),
                  pl.BlockSpec((None, tq, kvw), row),
                  pl.BlockSpec((None, BLOCK, kvw), prev_blk),
                  pl.BlockSpec((None, tq, kvw), row),
                  pl.BlockSpec((None, BLOCK, kvw), prev_blk),
                  pl.BlockSpec((None, tq, CONV_WIDTH), row),
                  pl.BlockSpec((None, HALO, CONV_WIDTH), prev_halo),
                  pl.BlockSpec((None, tq, D_MODEL), row),
                  pl.BlockSpec((N_HEADS, BLOCK, 2 * BLOCK), const3),
                  pl.BlockSpec((CONV_KERNEL, CONV_WIDTH), const2),
                  pl.BlockSpec((1, CONV_WIDTH), const2),
                  pl.BlockSpec((1, CONV_WIDTH), const2),
                  pl.BlockSpec((1, CONV_WIDTH), const2),
                  pl.BlockSpec((1, ATTN_WIDTH), const2),
                  pl.BlockSpec((1, CONV_WIDTH), const2),
                  pl.BlockSpec((MIX_WIDTH, D_MODEL), const2)],
        out_specs=pl.BlockSpec((None, tq, D_MODEL), row),
        scratch_shapes=[pltpu.VMEM((SUBLANES, HALO + tq, CONV_WIDTH), F32),
                        pltpu.VMEM((tq, MIX_WIDTH), BF16)],
        compiler_params=pltpu.CompilerParams(
            dimension_semantics=("arbitrary", "arbitrary"), vmem_limit_bytes=VMEM_LIMIT),
        name="mix",
    )(sinks, q, k, k, v, v, h, h, x, bias, cw, cb, lng, lnb, ag, cg, wo_bf16)


def _mlp_kernel(x_ref, g_ref, wu_ref, wd_ref, o_ref):
    x = x_ref[...]
    hn = _rms_rows(x, g_ref[...]).astype(BF16)
    u = jnp.dot(hn, wu_ref[...], preferred_element_type=F32)
    r = jnp.maximum(u, 0.0)
    r = (r * r).astype(BF16)
    o_ref[...] = x + jnp.dot(r, wd_ref[...], preferred_element_type=F32)


def _mlp(x, g, wu_bf16, wd_bf16):
    B, T, _ = x.shape
    tm = TM_MLP
    const = lambda b, i: (0, 0)
    row = lambda b, i: (b, i, 0)
    return pl.pallas_call(
        _mlp_kernel,
        out_shape=jax.ShapeDtypeStruct((B, T, D_MODEL), F32),
        grid=(B, T // tm),
        in_specs=[pl.BlockSpec((None, tm, D_MODEL), row),
                  pl.BlockSpec((1, D_MODEL), const),
                  pl.BlockSpec((D_MODEL, D_FF), const, pipeline_mode=pl.Buffered(1)),
                  pl.BlockSpec((D_FF, D_MODEL), const, pipeline_mode=pl.Buffered(1))],
        out_specs=pl.BlockSpec((None, tm, D_MODEL), row),
        compiler_params=pltpu.CompilerParams(
            dimension_semantics=("arbitrary", "arbitrary"), vmem_limit_bytes=VMEM_LIMIT),
        name="mlp",
    )(x, g, wu_bf16, wd_bf16)


def kernel(x, rel_bias, norm_mix_g, w_in, q_norm_g, k_norm_g, sinks, conv_w, conv_b,
           conv_ln_g, conv_ln_b, attn_out_g, conv_out_g, w_out, norm_mlp_g,
           w_mlp_up, w_mlp_down):
    depth = w_in.shape[0]
    bias = _bias_table(rel_bias)
    row = lambda a: a[None, :]
    for l in range(depth):
        gq = row(jnp.tile(q_norm_g[l], N_HEADS) * (HEAD_DIM ** -0.5))
        gk = row(jnp.tile(k_norm_g[l], N_KV_HEADS))
        q, k, v, h = _inproj(x, row(norm_mix_g[l]), w_in[l].astype(BF16), gq, gk)
        x = _mix(sinks[l], q, k, v, h, x, bias, conv_w[l], row(conv_b[l]),
                 row(conv_ln_g[l]), row(conv_ln_b[l]), row(attn_out_g[l]),
                 row(conv_out_g[l]), w_out[l].astype(BF16))
        x = _mlp(x, row(norm_mlp_g[l]), w_mlp_up[l].astype(BF16), w_mlp_down[l].astype(BF16))
    return x
```

```python
import math

import numpy as np
import jax
import jax.numpy as jnp
from jax import lax
from jax.experimental import pallas as pl
from jax.experimental.pallas import tpu as pltpu

D_MODEL = 1024
HEAD_DIM = 64
N_HEADS = 8
N_KV_HEADS = 2
GQA_GROUP = N_HEADS // N_KV_HEADS
ATTN_WIDTH = N_HEADS * HEAD_DIM
KV_WIDTH = N_KV_HEADS * HEAD_DIM
CONV_WIDTH = D_MODEL - ATTN_WIDTH
MIX_WIDTH = ATTN_WIDTH + CONV_WIDTH
IN_WIDTH = ATTN_WIDTH + 2 * KV_WIDTH + 2 * CONV_WIDTH
WINDOW = 128
BLOCK = 128
CONV_KERNEL = 31
NUM_BUCKETS = 32
MAX_DISTANCE = 128
D_FF = 4 * D_MODEL
EPS = 1e-6
NEG = -1e30
LOG2E = math.log2(math.e)

LANES = 128
SUBLANES = 8
HALO = 32
TM_PROJ = 512
TQ_MIX = 256
TM_MLP = 512
CONV_ROWS = 32
VMEM_LIMIT = 56 * 1024 * 1024

assert WINDOW == BLOCK and KV_WIDTH == LANES and HALO >= CONV_KERNEL - 1

F32 = jnp.float32
BF16 = jnp.bfloat16


def _t5_bucket(n):
    max_exact = NUM_BUCKETS // 2
    large = max_exact + (np.log(np.maximum(n, 1) / max_exact)
                         / np.log(MAX_DISTANCE / max_exact)
                         * (NUM_BUCKETS - max_exact)).astype(np.int32)
    large = np.minimum(large, NUM_BUCKETS - 1)
    return np.where(n < max_exact, n, large).astype(np.int32)


def _merged_bucket_table():
    j = np.arange(BLOCK)[:, None]
    qi = np.arange(BLOCK)[None, :]
    dist = np.where(j <= qi, qi - j, qi + BLOCK - j)
    return _t5_bucket(dist)


def _block_diag_ones(width, seg):
    r = np.arange(width)
    return (r[:, None] // seg == r[None, :] // seg).astype(np.float32)


def _bias_kernel(rel_ref, bucket_ref, out_ref):
    b = bucket_ref[...]
    for h in range(N_HEADS):
        acc = jnp.zeros((BLOCK, BLOCK), F32)
        for n in range(NUM_BUCKETS):
            acc = jnp.where(b == n, rel_ref[n, h] * LOG2E, acc)
        out_ref[h] = acc


def _bias_table(rel_bias):
    bucket = jnp.asarray(_merged_bucket_table())
    return pl.pallas_call(
        _bias_kernel,
        out_shape=jax.ShapeDtypeStruct((N_HEADS, BLOCK, BLOCK), F32),
        in_specs=[pl.BlockSpec(memory_space=pltpu.SMEM),
                  pl.BlockSpec(memory_space=pltpu.VMEM)],
        out_specs=pl.BlockSpec(memory_space=pltpu.VMEM),
        name="bias_table",
    )(rel_bias, bucket)


def _segment_mean_sq(z, ones_ref, seg):
    sq = z * z
    hi = sq.astype(BF16)
    lo = (sq - hi.astype(F32)).astype(BF16)
    ones = ones_ref[...]
    tot = (jnp.dot(hi, ones, preferred_element_type=F32)
           + jnp.dot(lo, ones, preferred_element_type=F32))
    return tot * (1.0 / seg)


def _inproj_kernel(x_ref, g_ref, w_ref, ones_q_ref, ones_k_ref, gq_ref, gk_ref,
                   qt_ref, k_ref, vt_ref, h_ref):
    x = x_ref[...]
    ms = jnp.mean(x * x, axis=-1, keepdims=True)
    hn = (x * lax.rsqrt(ms + EPS) * g_ref[...]).astype(BF16)
    tm = x.shape[0]

    zq = jnp.dot(hn, w_ref[:, 0:ATTN_WIDTH], preferred_element_type=F32)
    gq = gq_ref[...] * (HEAD_DIM ** -0.5 * LOG2E)
    qn = zq * lax.rsqrt(_segment_mean_sq(zq, ones_q_ref, HEAD_DIM) + EPS) * gq
    qnt = qn.T.astype(BF16)
    zeros = jnp.zeros((HEAD_DIM, tm), BF16)
    for h in range(N_HEADS):
        kvh = h // GQA_GROUP
        head = qnt[h * HEAD_DIM:(h + 1) * HEAD_DIM, :]
        live = h * LANES + kvh * HEAD_DIM
        dead = h * LANES + (1 - kvh) * HEAD_DIM
        qt_ref[live:live + HEAD_DIM, :] = head
        qt_ref[dead:dead + HEAD_DIM, :] = zeros

    c0 = ATTN_WIDTH
    zk = jnp.dot(hn, w_ref[:, c0:c0 + KV_WIDTH], preferred_element_type=F32)
    kn = zk * lax.rsqrt(_segment_mean_sq(zk, ones_k_ref, HEAD_DIM) + EPS) * gk_ref[...]
    k_ref[...] = kn.astype(BF16)

    c0 += KV_WIDTH
    zv = jnp.dot(hn, w_ref[:, c0:c0 + KV_WIDTH], preferred_element_type=F32)
    vt_ref[...] = zv.T.astype(BF16)

    c0 += KV_WIDTH
    u = jnp.dot(hn, w_ref[:, c0:c0 + CONV_WIDTH], preferred_element_type=F32)
    c0 += CONV_WIDTH
    gate = jnp.dot(hn, w_ref[:, c0:c0 + CONV_WIDTH], preferred_element_type=F32)
    h_ref[...] = u * (1.0 / (1.0 + jnp.exp(-gate)))


def _inproj(x, g, w_bf16, gq, gk):
    B, T, _ = x.shape
    tm = TM_PROJ
    ones_q = jnp.asarray(_block_diag_ones(ATTN_WIDTH, HEAD_DIM), BF16)
    ones_k = jnp.asarray(_block_diag_ones(KV_WIDTH, HEAD_DIM), BF16)
    const = lambda b, i: (0, 0)
    row = lambda b, i: (b, i, 0)
    col = lambda b, i: (b, 0, i)
    return pl.pallas_call(
        _inproj_kernel,
        out_shape=(jax.ShapeDtypeStruct((B, N_HEADS * LANES, T), BF16),
                   jax.ShapeDtypeStruct((B, T, KV_WIDTH), BF16),
                   jax.ShapeDtypeStruct((B, KV_WIDTH, T), BF16),
                   jax.ShapeDtypeStruct((B, T, CONV_WIDTH), F32)),
        grid=(B, T // tm),
        in_specs=[pl.BlockSpec((None, tm, D_MODEL), row),
                  pl.BlockSpec((1, D_MODEL), const),
                  pl.BlockSpec((D_MODEL, IN_WIDTH), const),
                  pl.BlockSpec((ATTN_WIDTH, ATTN_WIDTH), const),
                  pl.BlockSpec((KV_WIDTH, KV_WIDTH), const),
                  pl.BlockSpec((1, ATTN_WIDTH), const),
                  pl.BlockSpec((1, KV_WIDTH), const)],
        out_specs=(pl.BlockSpec((None, N_HEADS * LANES, tm), col),
                   pl.BlockSpec((None, tm, KV_WIDTH), row),
                   pl.BlockSpec((None, KV_WIDTH, tm), col),
                   pl.BlockSpec((None, tm, CONV_WIDTH), row)),
        compiler_params=pltpu.CompilerParams(
            dimension_semantics=("arbitrary", "arbitrary"), vmem_limit_bytes=VMEM_LIMIT),
        name="inproj",
    )(x, g, w_bf16, ones_q, ones_k, gq, gk)


def _rms_rows(y, g):
    ms = jnp.mean(y * y, axis=-1, keepdims=True)
    return y * lax.rsqrt(ms + EPS) * g


def _attention_block(sb, first, sinks_ref, qt_ref, kc_ref, kp_ref, vtc_ref, vtp_ref, bias_ref):
    r0 = sb * BLOCK
    j = lax.broadcasted_iota(jnp.int32, (BLOCK, BLOCK), 0)
    qi = lax.broadcasted_iota(jnp.int32, (BLOCK, BLOCK), 1)
    causal = j <= qi
    if sb == 0:
        k_prev, vt_prev = kp_ref[...], vtp_ref[...]
    else:
        k_prev, vt_prev = kc_ref[r0 - BLOCK:r0, :], vtc_ref[:, r0 - BLOCK:r0]
    kcat = jnp.concatenate([k_prev, kc_ref[r0:r0 + BLOCK, :]], axis=0)
    vtcat = jnp.concatenate([vt_prev, vtc_ref[:, r0:r0 + BLOCK]], axis=1)
    heads = []
    for kvh in range(N_KV_HEADS):
        h0 = kvh * GQA_GROUP
        qt = jnp.concatenate(
            [qt_ref[(h0 + g) * LANES:(h0 + g + 1) * LANES, r0:r0 + BLOCK]
             for g in range(GQA_GROUP)], axis=1)
        st = jnp.dot(kcat, qt, preferred_element_type=F32)
        probs, inv_denoms = [], []
        for g in range(GQA_GROUP):
            h = h0 + g
            s_prev = st[0:BLOCK, g * BLOCK:(g + 1) * BLOCK]
            s_cur = st[BLOCK:2 * BLOCK, g * BLOCK:(g + 1) * BLOCK]
            s = jnp.where(causal, s_cur, s_prev) + bias_ref[h]
            if sb == 0:
                s = jnp.where(causal | jnp.logical_not(first), s, NEG)
            sink = sinks_ref[h] * LOG2E
            m = jnp.maximum(jnp.max(s, axis=0, keepdims=True), sink)
            p = jnp.exp2(s - m)
            denom = jnp.sum(p, axis=0, keepdims=True) + jnp.exp2(sink - m)
            inv_denoms.append(1.0 / denom)
            probs.append(jnp.concatenate(
                [jnp.where(causal, 0.0, p), jnp.where(causal, p, 0.0)], axis=0).astype(BF16))
        pt = jnp.concatenate(probs, axis=1)
        ot = jnp.dot(vtcat, pt, preferred_element_type=F32)
        for g in range(GQA_GROUP):
            heads.append(ot[kvh * HEAD_DIM:(kvh + 1) * HEAD_DIM, g * BLOCK:(g + 1) * BLOCK]
                         * inv_denoms[g])
    return jnp.concatenate(heads, axis=0)


def _mix_kernel(sinks_ref, qt_ref, kc_ref, kp_ref, vtc_ref, vtp_ref, hc_ref, hp_ref, x_ref,
                bias_ref, cw_ref, cb_ref, lng_ref, lnb_ref, ag_ref, cg_ref, wo_ref,
                o_ref, hs_ref, cmix_ref):
    tq = x_ref.shape[0]
    first = pl.program_id(1) == 0

    hs_ref[0, 0:HALO, :] = jnp.where(first, 0.0, hp_ref[...])
    hs_ref[0, HALO:HALO + tq, :] = hc_ref[...]
    span = HALO + tq - SUBLANES
    hist = hs_ref[0]
    for s in range(1, SUBLANES):
        hs_ref[s, 0:span, :] = pltpu.roll(hist, HALO + tq - s, axis=0)[0:span, :]
    off = HALO - (CONV_KERNEL - 1)
    for c in range(tq // CONV_ROWS):
        r0 = c * CONV_ROWS
        groups = CONV_ROWS // SUBLANES
        acc = jnp.broadcast_to(cb_ref[...], (groups, SUBLANES, CONV_WIDTH))
        for t in range(CONV_KERNEL):
            a, s = divmod(off + t, SUBLANES)
            r = r0 + a * SUBLANES
            taps = hs_ref[s, r:r + CONV_ROWS, :].reshape(groups, SUBLANES, CONV_WIDTH)
            acc = acc + cw_ref[t][None] * taps
        acc = acc.reshape(CONV_ROWS, CONV_WIDTH)
        mu = jnp.mean(acc, axis=-1, keepdims=True)
        d = acc - mu
        var = jnp.mean(d * d, axis=-1, keepdims=True)
        y = d * lax.rsqrt(var + EPS) * lng_ref[...] + lnb_ref[...]
        y = y * (1.0 / (1.0 + jnp.exp(-y)))
        cmix_ref[r0:r0 + CONV_ROWS, :] = _rms_rows(y, cg_ref[...]).astype(BF16)

    conv_out = jnp.dot(cmix_ref[...], wo_ref[ATTN_WIDTH:MIX_WIDTH, :], preferred_element_type=F32)
    for sb in range(tq // BLOCK):
        r0 = sb * BLOCK
        at = _attention_block(sb, first, sinks_ref, qt_ref, kc_ref, kp_ref, vtc_ref, vtp_ref,
                              bias_ref)
        ms = jnp.mean(at * at, axis=0, keepdims=True)
        atn = (at * lax.rsqrt(ms + EPS) * ag_ref[...]).astype(BF16)
        attn_out = lax.dot_general(atn, wo_ref[0:ATTN_WIDTH, :], (((0,), (0,)), ((), ())),
                                   preferred_element_type=F32)
        o_ref[r0:r0 + BLOCK, :] = (x_ref[r0:r0 + BLOCK, :] + attn_out
                                   + conv_out[r0:r0 + BLOCK, :])


def _mix(sinks, qt, k, vt, h, x, bias, cw, cb, lng, lnb, ag_cols, cg, wo_bf16):
    B, T, _ = x.shape
    tq = TQ_MIX
    const2 = lambda b, i: (0, 0)
    const3 = lambda b, i: (0, 0, 0)
    row = lambda b, i: (b, i, 0)
    col = lambda b, i: (b, 0, i)
    prev_row_blk = lambda b, i: (b, jnp.maximum(i * (tq // BLOCK) - 1, 0), 0)
    prev_col_blk = lambda b, i: (b, 0, jnp.maximum(i * (tq // BLOCK) - 1, 0))
    prev_halo = lambda b, i: (b, jnp.maximum(i * (tq // HALO) - 1, 0), 0)
    return pl.pallas_call(
        _mix_kernel,
        out_shape=jax.ShapeDtypeStruct((B, T, D_MODEL), F32),
        grid=(B, T // tq),
        in_specs=[pl.BlockSpec(memory_space=pltpu.SMEM),
                  pl.BlockSpec((None, N_HEADS * LANES, tq), col),
                  pl.BlockSpec((None, tq, KV_WIDTH), row),
                  pl.BlockSpec((None, BLOCK, KV_WIDTH), prev_row_blk),
                  pl.BlockSpec((None, KV_WIDTH, tq), col),
                  pl.BlockSpec((None, KV_WIDTH, BLOCK), prev_col_blk),
                  pl.BlockSpec((None, tq, CONV_WIDTH), row),
                  pl.BlockSpec((None, HALO, CONV_WIDTH), prev_halo),
                  pl.BlockSpec((None, tq, D_MODEL), row),
                  pl.BlockSpec((N_HEADS, BLOCK, BLOCK), const3),
                  pl.BlockSpec((CONV_KERNEL, SUBLANES, CONV_WIDTH), const3),
                  pl.BlockSpec((1, CONV_WIDTH), const2),
                  pl.BlockSpec((1, CONV_WIDTH), const2),
                  pl.BlockSpec((1, CONV_WIDTH), const2),
                  pl.BlockSpec((ATTN_WIDTH, BLOCK), const2),
                  pl.BlockSpec((1, CONV_WIDTH), const2),
                  pl.BlockSpec((MIX_WIDTH, D_MODEL), const2)],
        out_specs=pl.BlockSpec((None, tq, D_MODEL), row),
        scratch_shapes=[pltpu.VMEM((SUBLANES, HALO + tq, CONV_WIDTH), F32),
                        pltpu.VMEM((tq, CONV_WIDTH), BF16)],
        compiler_params=pltpu.CompilerParams(
            dimension_semantics=("arbitrary", "arbitrary"), vmem_limit_bytes=VMEM_LIMIT),
        name="mix",
    )(sinks, qt, k, k, vt, vt, h, h, x, bias, cw, cb, lng, lnb, ag_cols, cg, wo_bf16)


def _mlp_kernel(x_ref, g_ref, wu_ref, wd_ref, o_ref):
    x = x_ref[...]
    hn = _rms_rows(x, g_ref[...]).astype(BF16)
    u = jnp.dot(hn, wu_ref[...], preferred_element_type=F32)
    r = jnp.maximum(u, 0.0)
    r = (r * r).astype(BF16)
    o_ref[...] = x + jnp.dot(r, wd_ref[...], preferred_element_type=F32)


def _mlp(x, g, wu_bf16, wd_bf16):
    B, T, _ = x.shape
    tm = TM_MLP
    const = lambda b, i: (0, 0)
    row = lambda b, i: (b, i, 0)
    return pl.pallas_call(
        _mlp_kernel,
        out_shape=jax.ShapeDtypeStruct((B, T, D_MODEL), F32),
        grid=(B, T // tm),
        in_specs=[pl.BlockSpec((None, tm, D_MODEL), row),
                  pl.BlockSpec((1, D_MODEL), const),
                  pl.BlockSpec((D_MODEL, D_FF), const, pipeline_mode=pl.Buffered(1)),
                  pl.BlockSpec((D_FF, D_MODEL), const, pipeline_mode=pl.Buffered(1))],
        out_specs=pl.BlockSpec((None, tm, D_MODEL), row),
        compiler_params=pltpu.CompilerParams(
            dimension_semantics=("arbitrary", "arbitrary"), vmem_limit_bytes=VMEM_LIMIT),
        name="mlp",
    )(x, g, wu_bf16, wd_bf16)


def kernel(x, rel_bias, norm_mix_g, w_in, q_norm_g, k_norm_g, sinks, conv_w, conv_b,
           conv_ln_g, conv_ln_b, attn_out_g, conv_out_g, w_out, norm_mlp_g,
           w_mlp_up, w_mlp_down):
    depth = w_in.shape[0]
    bias = _bias_table(rel_bias)
    row = lambda a: a[None, :]
    for l in range(depth):
        gq = row(jnp.tile(q_norm_g[l], N_HEADS))
        gk = row(jnp.tile(k_norm_g[l], N_KV_HEADS))
        ag_cols = jnp.broadcast_to(attn_out_g[l][:, None], (ATTN_WIDTH, BLOCK))
        qt, k, vt, h = _inproj(x, row(norm_mix_g[l]), w_in[l].astype(BF16), gq, gk)
        cw_rep = jnp.broadcast_to(conv_w[l][:, None, :], (CONV_KERNEL, SUBLANES, CONV_WIDTH))
        x = _mix(sinks[l], qt, k, vt, h, x, bias, cw_rep, row(conv_b[l]),
                 row(conv_ln_g[l]), row(conv_ln_b[l]), ag_cols,
                 row(conv_out_g[l]), w_out[l].astype(BF16))
        x = _mlp(x, row(norm_mlp_g[l]), w_mlp_up[l].astype(BF16), w_mlp_down[l].astype(BF16))
    return x
```
